```python
import math
import jax, jax.numpy as jnp
from jax import lax
import numpy as np

D_MODEL = 1024
BATCH = 4
SEQ = 4096
DEPTH = 4

CHUNK = 64
N_A_LAYERS = DEPTH // 2
N_B_LAYERS = DEPTH - N_A_LAYERS
POOL_WINDOWS = (2, 4, 8, 16)
N_POOL_GROUPS = len(POOL_WINDOWS)
POOL_GROUP_DIM = D_MODEL // N_POOL_GROUPS
D_FF = -(-8 * D_MODEL // (3 * 256)) * 256
N_HEADS = 8
QK_NOPE_DIM = 128
QK_ROPE_DIM = 64
QK_HEAD_DIM = QK_NOPE_DIM + QK_ROPE_DIM
V_HEAD_DIM = 128
Q_LORA_RANK = 256
KV_LORA_RANK = 512
ROPE_THETA = 10000.0
Q_BLOCK = 128
EPS = 1e-6

kernel_name = "yoco_pool_mla_hybrid"


def rms_norm(x, g):
    xf = x.astype(jnp.float32)
    y = xf * lax.rsqrt(jnp.mean(xf * xf, axis=-1, keepdims=True) + EPS)
    return (y * g.astype(jnp.float32)).astype(x.dtype)


def swiglu(h, w_gate, w_up, w_down):
    return (jax.nn.silu(h @ w_gate) * (h @ w_up)) @ w_down


def pool_mix(h, w_pool, b_pool, scale):
    B, S, D = h.shape
    hf = h.astype(jnp.float32)
    cs0 = jnp.pad(jnp.cumsum(hf, axis=1), ((0, 0), (1, 0), (0, 0)))
    t1 = jnp.arange(1, S + 1)
    groups = []
    for g, w in enumerate(POOL_WINDOWS):
        sl = slice(g * POOL_GROUP_DIM, (g + 1) * POOL_GROUP_DIM)
        c = cs0[:, :, sl]
        lag = jnp.pad(c, ((0, 0), (w, 0), (0, 0)))[:, : S + 1]
        win = (c - lag)[:, 1:]
        cnt = jnp.minimum(t1, w).astype(jnp.float32)[None, :, None]
        groups.append(win / cnt - hf[:, :, sl])
    d = jnp.stack(groups, axis=2).astype(h.dtype)
    y = jnp.einsum('bsgc,gcd->bsgd', d, w_pool) + b_pool
    return y.reshape(B, S, D) * scale


def rope(x, positions):
    R = x.shape[-1]
    half = R // 2
    inv = ROPE_THETA ** (-jnp.arange(half, dtype=jnp.float32) * 2.0 / R)
    ang = positions.astype(jnp.float32)[..., None] * inv
    cos = jnp.cos(ang)[:, :, None, :]
    sin = jnp.sin(ang)[:, :, None, :]
    xf = x.astype(jnp.float32)
    x1, x2 = xf[..., :half], xf[..., half:]
    return jnp.concatenate([x1 * cos - x2 * sin, x2 * cos + x1 * sin], axis=-1).astype(x.dtype)


def shared_kv(h, positions, w_dkv, g_kv_latent, w_uk, w_uv, g_k):
    B, S, _ = h.shape
    ckv = h @ w_dkv
    c = rms_norm(ckv[..., :KV_LORA_RANK], g_kv_latent)
    k_pe = ckv[..., KV_LORA_RANK:]
    k_nope = (c @ w_uk).reshape(B, S, N_HEADS, QK_NOPE_DIM)
    v = (c @ w_uv).reshape(B, S, N_HEADS, V_HEAD_DIM)
    k = jnp.concatenate([k_nope, jnp.broadcast_to(k_pe[:, :, None, :], (B, S, N_HEADS, QK_ROPE_DIM))], axis=-1)
    k = rms_norm(k, g_k)
    k = jnp.concatenate([k[..., :QK_NOPE_DIM], rope(k[..., QK_NOPE_DIM:], positions)], axis=-1)
    return k, v


def mla_queries(h, positions, w_dq, g_q_latent, w_uq, g_q):
    B, S, _ = h.shape
    cq = rms_norm(h @ w_dq, g_q_latent)
    q = (cq @ w_uq).reshape(B, S, N_HEADS, QK_HEAD_DIM)
    q = rms_norm(q, g_q)
    return jnp.concatenate([q[..., :QK_NOPE_DIM], rope(q[..., QK_NOPE_DIM:], positions)], axis=-1)


def chunk_causal_attention(q, k, v):
    S = q.shape[1]
    scale = 1.0 / math.sqrt(QK_HEAD_DIM)
    outs = []
    for i in range(S // Q_BLOCK):
        start, end = i * Q_BLOCK, (i + 1) * Q_BLOCK
        s = jnp.einsum('bqhd,bkhd->bhqk', q[:, start:end], k[:, :end]).astype(jnp.float32) * scale
        qc = (start + jnp.arange(Q_BLOCK)) // CHUNK
        kc = jnp.arange(end) // CHUNK
        mask = kc[None, :] <= qc[:, None]
        s = jnp.where(mask[None, None], s, jnp.float32(-1e30))
        p = jax.nn.softmax(s, axis=-1).astype(v.dtype)
        outs.append(jnp.einsum('bhqk,bkhd->bqhd', p, v[:, :end]))
    return jnp.concatenate(outs, axis=1)


def setup_inputs(seed: int = 0) -> dict:
    key = jax.random.key(seed)
    ks = iter(jax.random.split(key, 32))
    f32 = jnp.float32

    def w(shape, fan_in):
        return jax.random.normal(next(ks), shape, f32) * fan_in ** -0.5

    def gain(shape):
        return 1.0 + 0.02 * jax.random.normal(next(ks), shape, f32)

    x = jax.random.normal(next(ks), (BATCH, SEQ, D_MODEL), f32)
    positions = jnp.broadcast_to(jnp.arange(SEQ, dtype=jnp.int32)[None, :], (BATCH, SEQ)).astype(jnp.int32)
    return {
        "x": x,
        "positions": positions,
        "ln_mix_a": gain((N_A_LAYERS, D_MODEL)),
        "w_pool": w((N_A_LAYERS, N_POOL_GROUPS, POOL_GROUP_DIM, POOL_GROUP_DIM), POOL_GROUP_DIM),
        "b_pool": 0.02 * jax.random.normal(next(ks), (N_A_LAYERS, N_POOL_GROUPS, POOL_GROUP_DIM), f32),
        "pool_scale": gain((N_A_LAYERS, D_MODEL)),
        "ln_ffn": gain((DEPTH, D_MODEL)),
        "w_gate": w((DEPTH, D_MODEL, D_FF), D_MODEL),
        "w_up": w((DEPTH, D_MODEL, D_FF), D_MODEL),
        "w_down": w((DEPTH, D_FF, D_MODEL), D_FF),
        "ln_kv": gain((D_MODEL,)),
        "w_dkv": w((D_MODEL, KV_LORA_RANK + QK_ROPE_DIM), D_MODEL),
        "g_kv_latent": gain((KV_LORA_RANK,)),
        "w_uk": w((KV_LORA_RANK, N_HEADS * QK_NOPE_DIM), KV_LORA_RANK),
        "w_uv": w((KV_LORA_RANK, N_HEADS * V_HEAD_DIM), KV_LORA_RANK),
        "g_k": gain((QK_HEAD_DIM,)),
        "ln_mix_b": gain((N_B_LAYERS, D_MODEL)),
        "w_dq": w((N_B_LAYERS, D_MODEL, Q_LORA_RANK), D_MODEL),
        "g_q_latent": gain((N_B_LAYERS, Q_LORA_RANK)),
        "w_uq": w((N_B_LAYERS, Q_LORA_RANK, N_HEADS * QK_HEAD_DIM), Q_LORA_RANK),
        "g_q": gain((N_B_LAYERS, QK_HEAD_DIM)),
        "w_o": w((N_B_LAYERS, N_HEADS * V_HEAD_DIM, D_MODEL), N_HEADS * V_HEAD_DIM),
    }


def reference(x, positions, ln_mix_a, w_pool, b_pool, pool_scale, ln_ffn, w_gate, w_up, w_down,
              ln_kv, w_dkv, g_kv_latent, w_uk, w_uv, g_k,
              ln_mix_b, w_dq, g_q_latent, w_uq, g_q, w_o):
    B, S, D = x.shape
    k_sh, v_sh = None, None
    for l in range(DEPTH):
        if l < N_A_LAYERS:
            x = x + pool_mix(rms_norm(x, ln_mix_a[l]), w_pool[l], b_pool[l], pool_scale[l])
        else:
            j = l - N_A_LAYERS
            q = mla_queries(rms_norm(x, ln_mix_b[j]), positions, w_dq[j], g_q_latent[j], w_uq[j], g_q[j])
            o = chunk_causal_attention(q, k_sh, v_sh)
            x = x + o.reshape(B, S, N_HEADS * V_HEAD_DIM) @ w_o[j]
        x = x + swiglu(rms_norm(x, ln_ffn[l]), w_gate[l], w_up[l], w_down[l])
        if l == N_A_LAYERS - 1:
            k_sh, v_sh = shared_kv(rms_norm(x, ln_kv), positions, w_dkv, g_kv_latent, w_uk, w_uv, g_k)
    return x
```

```python
import functools
import math

import jax
import jax.numpy as jnp
from jax import lax
from jax.experimental import pallas as pl
from jax.experimental.pallas import tpu as pltpu

D_MODEL = 1024
DEPTH = 4
CHUNK = 64
N_A_LAYERS = DEPTH // 2
POOL_WINDOWS = (2, 4, 8, 16)
N_POOL_GROUPS = len(POOL_WINDOWS)
POOL_GROUP_DIM = D_MODEL // N_POOL_GROUPS
D_FF = 2816
N_HEADS = 8
QK_NOPE_DIM = 128
QK_ROPE_DIM = 64
QK_HEAD_DIM = QK_NOPE_DIM + QK_ROPE_DIM
V_HEAD_DIM = 128
Q_LORA_RANK = 256
KV_LORA_RANK = 512
ROPE_THETA = 10000.0
EPS = 1e-6
MASK_VALUE = -1e30

LANES = 128
HEAD_PAD = 2 * LANES
ROPE_HALF = QK_ROPE_DIM // 2
POOL_HALO = 16

FFN_TOKENS = 512
FFN_CHUNK = 256
POOL_TOKENS = 512
PROJ_TOKENS = 512
ATTN_Q = 256
ATTN_K = 256

F32 = jnp.float32
BF16 = jnp.bfloat16


def _rms(x, g):
    ms = jnp.mean(x * x, axis=-1, keepdims=True)
    return x * lax.rsqrt(ms + EPS) * g


def _const_spec(shape):
    zeros = (0,) * len(shape)
    return pl.BlockSpec(shape, lambda *_: zeros, pipeline_mode=pl.Buffered(1))


def _params(sem, vmem_mb):
    return pltpu.CompilerParams(dimension_semantics=sem, vmem_limit_bytes=vmem_mb * 1024 * 1024)


def _ffn_kernel(*refs, with_attn_proj):
    if with_attn_proj:
        x_ref, o_ref, wo_ref, ln_ref, wg_ref, wu_ref, wd_ref, out_ref, acc_ref = refs
        x = x_ref[...] + jnp.dot(o_ref[...], wo_ref[...], preferred_element_type=F32)
    else:
        x_ref, ln_ref, wg_ref, wu_ref, wd_ref, out_ref, acc_ref = refs
        x = x_ref[...]
    h = _rms(x, ln_ref[...]).astype(BF16)
    for c in range(D_FF // FFN_CHUNK):
        sl = slice(c * FFN_CHUNK, (c + 1) * FFN_CHUNK)
        g = jnp.dot(h, wg_ref[:, sl], preferred_element_type=F32)
        u = jnp.dot(h, wu_ref[:, sl], preferred_element_type=F32)
        a = (g * jax.nn.sigmoid(g) * u).astype(BF16)
        d = jnp.dot(a, wd_ref[sl, :], preferred_element_type=F32)
        if c == 0:
            acc_ref[...] = d
        else:
            acc_ref[...] += d
    out_ref[...] = x + acc_ref[...]


def _ffn(x, ln, wg, wu, wd, o=None, wo=None):
    t = x.shape[0]
    tm = FFN_TOKENS
    tok = lambda w: pl.BlockSpec((tm, w), lambda i: (i, 0))
    in_specs = [tok(D_MODEL)]
    args = [x]
    if o is not None:
        in_specs += [tok(N_HEADS * V_HEAD_DIM), _const_spec(wo.shape)]
        args += [o, wo]
    in_specs += [_const_spec(ln.shape), _const_spec(wg.shape), _const_spec(wu.shape), _const_spec(wd.shape)]
    args += [ln, wg, wu, wd]
    return pl.pallas_call(
        functools.partial(_ffn_kernel, with_attn_proj=o is not None),
        grid=(t // tm,),
        in_specs=in_specs,
        out_specs=tok(D_MODEL),
        out_shape=jax.ShapeDtypeStruct((t, D_MODEL), F32),
        scratch_shapes=[pltpu.VMEM((tm, D_MODEL), F32)],
        compiler_params=_params(("parallel",), 48),
        name="ffn_attn" if o is not None else "ffn",
    )(*args)


def _pool_kernel(x_ref, halo_ref, ln_ref, wp_ref, b_ref, sc_ref, out_ref, hbuf):
    s = pl.program_id(1)
    ts = x_ref.shape[1]
    ln = ln_ref[...]
    x = x_ref[0]
    hbuf[POOL_HALO:, :] = _rms(x, ln)
    hbuf[:POOL_HALO, :] = jnp.where(s > 0, _rms(halo_ref[0], ln), 0.0)
    t1 = s * ts + 1 + lax.broadcasted_iota(jnp.int32, (ts, 1), 0)
    for g, w in enumerate(POOL_WINDOWS):
        cols = slice(g * POOL_GROUP_DIM, (g + 1) * POOL_GROUP_DIM)
        cur = hbuf[POOL_HALO:, cols]
        win = cur
        for k in range(1, w):
            win = win + hbuf[POOL_HALO - k:POOL_HALO - k + ts, cols]
        inv_cnt = 1.0 / jnp.minimum(t1, w).astype(F32)
        d = (win * inv_cnt - cur).astype(BF16)
        y = jnp.dot(d, wp_ref[g], preferred_element_type=F32) + b_ref[:, cols]
        out_ref[0, :, cols] = x[:, cols] + y * sc_ref[:, cols]


def _pool(x3, ln, wp, b, sc):
    bsz, s, _ = x3.shape
    ts = POOL_TOKENS
    halo_per_tile = ts // POOL_HALO
    return pl.pallas_call(
        _pool_kernel,
        grid=(bsz, s // ts),
        in_specs=[
            pl.BlockSpec((1, ts, D_MODEL), lambda b_, i: (b_, i, 0)),
            pl.BlockSpec((1, POOL_HALO, D_MODEL),
                         lambda b_, i: (b_, jnp.maximum(i * halo_per_tile - 1, 0), 0)),
            _const_spec(ln.shape), _const_spec(wp.shape), _const_spec(b.shape), _const_spec(sc.shape),
        ],
        out_specs=pl.BlockSpec((1, ts, D_MODEL), lambda b_, i: (b_, i, 0)),
        out_shape=jax.ShapeDtypeStruct(x3.shape, F32),
        scratch_shapes=[pltpu.VMEM((ts + POOL_HALO, D_MODEL), F32)],
        compiler_params=_params(("parallel", "parallel"), 32),
        name="pool_mix",
    )(x3, x3, ln, wp, b, sc)


def _cos_sin(pos_ref, inv_ref):
    ang = pos_ref[...].astype(F32) * inv_ref[...]
    return jnp.cos(ang), jnp.sin(ang)


def _rope(v, cos, sin):
    lane = lax.broadcasted_iota(jnp.int32, v.shape, 1)
    first_half = (lane % QK_ROPE_DIM) < ROPE_HALF
    rot = jnp.where(first_half, -pltpu.roll(v, LANES - ROPE_HALF, 1), pltpu.roll(v, ROPE_HALF, 1))
    return v * cos + rot * sin


def _kv_kernel(x_ref, pos_ref, inv_ref, ln_ref, wdkv_ref, gl_ref, wuk_ref, wuv_ref, gk_ref, k_ref, v_ref):
    h = _rms(x_ref[0], ln_ref[...]).astype(BF16)
    ckv = jnp.dot(h, wdkv_ref[...], preferred_element_type=F32)
    c = _rms(ckv[:, :KV_LORA_RANK], gl_ref[...]).astype(BF16)
    pe = ckv[:, KV_LORA_RANK:]
    pe_ss = jnp.sum(pe * pe, axis=-1, keepdims=True)
    k_nope = jnp.dot(c, wuk_ref[...], preferred_element_type=F32)
    v = jnp.dot(c, wuv_ref[...], preferred_element_type=F32)
    cos, sin = _cos_sin(pos_ref.at[0], inv_ref)
    g_nope = gk_ref[:, :QK_NOPE_DIM]
    g_pe = gk_ref[:, QK_NOPE_DIM:]
    for hd in range(N_HEADS):
        kn = k_nope[:, hd * QK_NOPE_DIM:(hd + 1) * QK_NOPE_DIM]
        ss = jnp.sum(kn * kn, axis=-1, keepdims=True) + pe_ss
        r = lax.rsqrt(ss * (1.0 / QK_HEAD_DIM) + EPS)
        k_ref[0, hd, :, :QK_NOPE_DIM] = (kn * r * g_nope).astype(BF16)
        k_ref[0, hd, :, QK_NOPE_DIM:] = _rope(pe * r * g_pe, cos, sin).astype(BF16)
        v_ref[0, hd] = v[:, hd * V_HEAD_DIM:(hd + 1) * V_HEAD_DIM].astype(BF16)


def _shared_kv(x3, pos3, inv, ln, wdkv, gl, wuk, wuv, gk):
    bsz, s, _ = x3.shape
    tm = PROJ_TOKENS
    return pl.pallas_call(
        _kv_kernel,
        grid=(bsz, s // tm),
        in_specs=[
            pl.BlockSpec((1, tm, D_MODEL), lambda b_, i: (b_, i, 0)),
            pl.BlockSpec((1, tm, 1), lambda b_, i: (b_, i, 0)),
            _const_spec(inv.shape), _const_spec(ln.shape), _const_spec(wdkv.shape), _const_spec(gl.shape),
            _const_spec(wuk.shape), _const_spec(wuv.shape), _const_spec(gk.shape),
        ],
        out_specs=[
            pl.BlockSpec((1, N_HEADS, tm, HEAD_PAD), lambda b_, i: (b_, 0, i, 0)),
            pl.BlockSpec((1, N_HEADS, tm, V_HEAD_DIM), lambda b_, i: (b_, 0, i, 0)),
        ],
        out_shape=[
            jax.ShapeDtypeStruct((bsz, N_HEADS, s, HEAD_PAD), BF16),
            jax.ShapeDtypeStruct((bsz, N_HEADS, s, V_HEAD_DIM), BF16),
        ],
        compiler_params=_params(("parallel", "parallel"), 32),
        name="shared_kv",
    )(x3, pos3, inv, ln, wdkv, gl, wuk, wuv, gk)


def _q_kernel(x_ref, pos_ref, inv_ref, ln_ref, wdq_ref, gl_ref, wuq_ref, gq_ref, q_ref):
    h = _rms(x_ref[0], ln_ref[...]).astype(BF16)
    cq = _rms(jnp.dot(h, wdq_ref[...], preferred_element_type=F32), gl_ref[...]).astype(BF16)
    q = jnp.dot(cq, wuq_ref[...], preferred_element_type=F32)
    cos, sin = _cos_sin(pos_ref.at[0], inv_ref)
    g_nope = gq_ref[:, :QK_NOPE_DIM]
    g_pe = gq_ref[:, QK_NOPE_DIM:]
    for hd in range(N_HEADS):
        qn = q[:, hd * HEAD_PAD:hd * HEAD_PAD + QK_NOPE_DIM]
        qp = q[:, hd * HEAD_PAD + QK_NOPE_DIM:(hd + 1) * HEAD_PAD]
        ss = jnp.sum(qn * qn, axis=-1, keepdims=True) + jnp.sum(qp * qp, axis=-1, keepdims=True)
        r = lax.rsqrt(ss * (1.0 / QK_HEAD_DIM) + EPS)
        q_ref[0, hd, :, :QK_NOPE_DIM] = (qn * r * g_nope).astype(BF16)
        q_ref[0, hd, :, QK_NOPE_DIM:] = _rope(qp * r * g_pe, cos, sin).astype(BF16)


def _queries(x3, pos3, inv, ln, wdq, gl, wuq, gq):
    bsz, s, _ = x3.shape
    tm = PROJ_TOKENS
    return pl.pallas_call(
        _q_kernel,
        grid=(bsz, s // tm),
        in_specs=[
            pl.BlockSpec((1, tm, D_MODEL), lambda b_, i: (b_, i, 0)),
            pl.BlockSpec((1, tm, 1), lambda b_, i: (b_, i, 0)),
            _const_spec(inv.shape), _const_spec(ln.shape), _const_spec(wdq.shape), _const_spec(gl.shape),
            _const_spec(wuq.shape), _const_spec(gq.shape),
        ],
        out_specs=pl.BlockSpec((1, N_HEADS, tm, HEAD_PAD), lambda b_, i: (b_, 0, i, 0)),
        out_shape=jax.ShapeDtypeStruct((bsz, N_HEADS, s, HEAD_PAD), BF16),
        compiler_params=_params(("parallel", "parallel"), 32),
        name="mla_queries",
    )(x3, pos3, inv, ln, wdq, gl, wuq, gq)


def _attn_kernel(q_ref, k_ref, v_ref, o_ref, m_ref, l_ref, acc_ref):
    i = pl.program_id(2)
    q = q_ref[0, 0]
    m_ref[...] = jnp.full_like(m_ref, -jnp.inf)
    l_ref[...] = jnp.zeros_like(l_ref)
    acc_ref[...] = jnp.zeros_like(acc_ref)

    def step(j, masked):
        start = pl.multiple_of(j * ATTN_K, ATTN_K)
        k = k_ref[0, 0, pl.ds(start, ATTN_K), :]
        v = v_ref[0, 0, pl.ds(start, ATTN_K), :]
        s = lax.dot_general(q, k, (((1,), (1,)), ((), ())), preferred_element_type=F32)
        if masked:
            qc = lax.broadcasted_iota(jnp.int32, s.shape, 0) // CHUNK
            kc = lax.broadcasted_iota(jnp.int32, s.shape, 1) // CHUNK
            s = jnp.where(kc <= qc, s, MASK_VALUE)
        m_prev = m_ref[...]
        m_new = jnp.maximum(m_prev, jnp.max(s, axis=-1, keepdims=True))
        alpha = jnp.exp(m_prev - m_new)
        p = jnp.exp(s - m_new)
        l_ref[...] = alpha * l_ref[...] + jnp.sum(p, axis=-1, keepdims=True)
        acc_ref[...] = alpha * acc_ref[...] + jnp.dot(p.astype(BF16), v, preferred_element_type=F32)
        m_ref[...] = m_new

    def body(j, carry):
        step(j, False)
        return carry

    lax.fori_loop(0, i, body, 0)
    step(i, True)
    o_ref[0] = (acc_ref[...] / l_ref[...]).astype(o_ref.dtype)


def _attention(q, k, v):
    bsz, nh, s, _ = q.shape
    assert ATTN_Q == ATTN_K and ATTN_Q % CHUNK == 0
    return pl.pallas_call(
        _attn_kernel,
        grid=(bsz, nh, s // ATTN_Q),
        in_specs=[
            pl.BlockSpec((1, 1, ATTN_Q, HEAD_PAD), lambda b_, h_, i: (b_, h_, i, 0)),
            pl.BlockSpec((1, 1, s, HEAD_PAD), lambda b_, h_, i: (b_, h_, 0, 0)),
            pl.BlockSpec((1, 1, s, V_HEAD_DIM), lambda b_, h_, i: (b_, h_, 0, 0)),
        ],
        out_specs=pl.BlockSpec((1, ATTN_Q, V_HEAD_DIM), lambda b_, h_, i: (b_, i, h_)),
        out_shape=jax.ShapeDtypeStruct((bsz, s, nh * V_HEAD_DIM), BF16),
        scratch_shapes=[
            pltpu.VMEM((ATTN_Q, 1), F32),
            pltpu.VMEM((ATTN_Q, 1), F32),
            pltpu.VMEM((ATTN_Q, V_HEAD_DIM), F32),
        ],
        compiler_params=_params(("parallel", "parallel", "arbitrary"), 32),
        name="chunk_causal_attention",
    )(q, k, v)


def _row(v):
    return v.reshape(1, -1).astype(F32)


def _pad_head_cols(w):
    lead = w.shape[:-1]
    w = w.reshape(*lead, N_HEADS, QK_HEAD_DIM)
    w = jnp.pad(w, [(0, 0)] * len(lead) + [(0, 0), (0, HEAD_PAD - QK_HEAD_DIM)])
    return w.reshape(*lead, N_HEADS * HEAD_PAD)


def _pad_head_gain(g, scale=1.0):
    return jnp.pad(g.astype(F32) * scale, (0, HEAD_PAD - QK_HEAD_DIM)).reshape(1, HEAD_PAD)


def kernel(x, positions, ln_mix_a, w_pool, b_pool, pool_scale, ln_ffn, w_gate, w_up, w_down,
           ln_kv, w_dkv, g_kv_latent, w_uk, w_uv, g_k,
           ln_mix_b, w_dq, g_q_latent, w_uq, g_q, w_o):
    bsz, s, d = x.shape
    t = bsz * s
    pos3 = positions.reshape(bsz, s, 1)
    inv = ROPE_THETA ** (-jnp.arange(ROPE_HALF, dtype=F32) * 2.0 / QK_ROPE_DIM)
    inv = jnp.tile(inv, LANES // ROPE_HALF).reshape(1, LANES)

    wg, wu, wd = w_gate.astype(BF16), w_up.astype(BF16), w_down.astype(BF16)
    wdkv = jnp.pad(w_dkv, ((0, 0), (0, LANES - QK_ROPE_DIM))).astype(BF16)
    wuq = _pad_head_cols(w_uq).astype(BF16)
    softmax_scale = 1.0 / math.sqrt(QK_HEAD_DIM)

    k_sh = v_sh = None
    for l in range(DEPTH):
        if l < N_A_LAYERS:
            x = _pool(x, _row(ln_mix_a[l]), w_pool[l].astype(BF16), _row(b_pool[l]), _row(pool_scale[l]))
            x2 = _ffn(x.reshape(t, d), _row(ln_ffn[l]), wg[l], wu[l], wd[l])
        else:
            j = l - N_A_LAYERS
            q = _queries(x, pos3, inv, _row(ln_mix_b[j]), w_dq[j].astype(BF16), _row(g_q_latent[j]),
                         wuq[j], _pad_head_gain(g_q[j], softmax_scale))
            o = _attention(q, k_sh, v_sh)
            x2 = _ffn(x.reshape(t, d), _row(ln_ffn[l]), wg[l], wu[l], wd[l],
                      o=o.reshape(t, N_HEADS * V_HEAD_DIM), wo=w_o[j].astype(BF16))
        x = x2.reshape(bsz, s, d)
        if l == N_A_LAYERS - 1:
            k_sh, v_sh = _shared_kv(x, pos3, inv, _row(ln_kv), wdkv, _row(g_kv_latent),
                                    w_uk.astype(BF16), w_uv.astype(BF16), _pad_head_gain(g_k))
    return x
```
